```python
import math
import jax, jax.numpy as jnp
from jax import lax
import numpy as np

D_MODEL = 4096
BATCH = 4
SEQ = 2048
DEPTH = 4
DEC_BATCH = 128
DEC_SEQ = 1
PAST_LEN = 16384
PAGE_SIZE = 128

N_EVEN = (DEPTH + 1) // 2
N_ODD = DEPTH // 2
BRANCH = D_MODEL
MIX = 2 * BRANCH
EPS = 1e-6
S5_H = 16
S5_P = 64
S5_G = BRANCH // S5_H
S5_CHUNK = 128
ML_HEADS = 8
ML_DQK = BRANCH // ML_HEADS // 2
ML_DV = BRANCH // ML_HEADS
ML_CHUNK = 64
RG_CONV = 4
RG_BLOCKS = 16
RG_BS = BRANCH // RG_BLOCKS
RG_C = 8.0
RET_HEADS = 16
RET_DK = BRANCH // RET_HEADS
RET_DV = BRANCH // RET_HEADS
RET_CHUNK = 64
ROPE_BASE = 10000.0

SPLIT_EVEN = (BRANCH, BRANCH, ML_HEADS * ML_DQK, ML_HEADS * ML_DQK, BRANCH, BRANCH, ML_HEADS, ML_HEADS, BRANCH)
SPLIT_ODD = (BRANCH, BRANCH, RET_HEADS * RET_DK, RET_HEADS * RET_DK, RET_HEADS * RET_DV, BRANCH)
IN_EVEN = sum(SPLIT_EVEN)
IN_ODD = sum(SPLIT_ODD)

kernel_name = 'hybrid_s5_mlstm_rglru_retention_step'


def _split(t, sizes):
    offs = [int(v) for v in np.cumsum(sizes)[:-1]]
    return jnp.split(t, offs, axis=-1)


def _rmsnorm(x, g):
    xf = x.astype(jnp.float32)
    return xf * lax.rsqrt(jnp.mean(xf * xf, axis=-1, keepdims=True) + EPS) * g.astype(jnp.float32)


def _head_rmsnorm(t, g):
    b, s, h, dh = t.shape
    t = t * lax.rsqrt(jnp.mean(t * t, axis=-1, keepdims=True) + EPS)
    return t.reshape(b, s, h * dh) * g.astype(jnp.float32)


def _head_layernorm(t, g):
    b, s, h, dh = t.shape
    tc = t - jnp.mean(t, axis=-1, keepdims=True)
    t = tc * lax.rsqrt(jnp.mean(tc * tc, axis=-1, keepdims=True) + EPS)
    return t.reshape(b, s, h * dh) * g.astype(jnp.float32)


def _to_chunks(t, n, L):
    return jnp.moveaxis(t.reshape((t.shape[0], n, L) + t.shape[2:]), 1, 0)


def _from_chunks(t):
    t = jnp.moveaxis(t, 0, 1)
    return t.reshape((t.shape[0], t.shape[1] * t.shape[2]) + t.shape[3:])


def _cplx_combine(e, l):
    ar1, ai1, br1, bi1 = e
    ar2, ai2, br2, bi2 = l
    return (ar2 * ar1 - ai2 * ai1, ar2 * ai1 + ai2 * ar1,
            ar2 * br1 - ai2 * bi1 + br2, ar2 * bi1 + ai2 * br1 + bi2)


def _lin_combine(e, l):
    return (l[0] * e[0], l[0] * e[1] + l[1])


def _s5(u, st_re, st_im, lam_re, lam_im, b_re, b_im, c_re, c_im, d, log_step, glu_w, glu_b):
    f32 = jnp.float32
    bsz, s, _ = u.shape
    L = math.gcd(s, S5_CHUNK)
    n = s // L
    lam_re = lam_re.astype(f32)
    lam_im = lam_im.astype(f32)
    b_re = b_re.astype(f32)
    b_im = b_im.astype(f32)
    c_re = c_re.astype(f32)
    c_im = c_im.astype(f32)
    step = jnp.exp(log_step.astype(f32))[:, None]
    mag = jnp.exp(lam_re * step)
    ang = lam_im * step
    abar_re = mag * jnp.cos(ang)
    abar_im = mag * jnp.sin(ang)
    nr = abar_re - 1.0
    ni = abar_im
    den = lam_re * lam_re + lam_im * lam_im
    fr = (nr * lam_re + ni * lam_im) / den
    fi = (ni * lam_re - nr * lam_im) / den
    bbar_re = fr[..., None] * b_re - fi[..., None] * b_im
    bbar_im = fr[..., None] * b_im + fi[..., None] * b_re
    uc = _to_chunks(u.reshape(bsz, s, S5_G, S5_H), n, L)

    def block(carry, u_blk):
        xr0, xi0 = carry
        bur = jnp.einsum('blgh,gph->blgp', u_blk, bbar_re)
        bui = jnp.einsum('blgh,gph->blgp', u_blk, bbar_im)
        ar = jnp.broadcast_to(abar_re, bur.shape)
        ai = jnp.broadcast_to(abar_im, bur.shape)
        a_r, a_i, x_r, x_i = lax.associative_scan(_cplx_combine, (ar, ai, bur, bui), axis=1)
        xr = a_r * xr0[:, None] - a_i * xi0[:, None] + x_r
        xi = a_r * xi0[:, None] + a_i * xr0[:, None] + x_i
        y = jnp.einsum('blgp,ghp->blgh', xr, c_re) - jnp.einsum('blgp,ghp->blgh', xi, c_im)
        return (xr[:, -1], xi[:, -1]), y

    (sr, si), y = lax.scan(block, (st_re, st_im), uc)
    y = _from_chunks(y).reshape(bsz, s, BRANCH) + d.astype(f32) * u
    g = jax.nn.gelu(y)
    return g * jax.nn.sigmoid(g @ glu_w + glu_b), sr, si


def _mlstm(q, k, v, logi, f_pre, c0, n0, m0):
    bsz, s = q.shape[0], q.shape[1]
    L = math.gcd(s, ML_CHUNK)
    n = s // L
    k = k * (ML_DQK ** -0.5)
    logf = jax.nn.log_sigmoid(f_pre)
    causal = jnp.tril(jnp.ones((L, L), dtype=bool))[None, :, :, None]
    xs = tuple(_to_chunks(t, n, L) for t in (q, k, v, logi, logf))

    def block(carry, blk):
        c, nv, m = carry
        qb, kb, vb, lib, lfb = blk
        b = jnp.cumsum(lfb, axis=1)
        dmat = jnp.where(causal, b[:, :, None, :] - b[:, None, :, :] + lib[:, None, :, :], -jnp.inf)
        inter = b + m[:, None, :]
        m_t = jnp.maximum(inter, jnp.max(dmat, axis=2))
        w = jnp.exp(dmat - m_t[:, :, None, :])
        w_inter = jnp.exp(inter - m_t)
        qk = jnp.einsum('bthd,bshd->btsh', qb, kb) * w
        num = jnp.einsum('btsh,bshv->bthv', qk, vb) + w_inter[..., None] * jnp.einsum('bthd,bhdv->bthv', qb, c)
        den = jnp.sum(qk, axis=2) + w_inter * jnp.einsum('bthd,bhd->bth', qb, nv)
        h = num / jnp.maximum(jnp.abs(den), jnp.exp(-m_t))[..., None]
        m_new = m_t[:, -1]
        ws = jnp.exp(b[:, -1:] - b + lib - m_new[:, None])
        wc = jnp.exp(b[:, -1] + m - m_new)
        kw = kb * ws[..., None]
        c_new = wc[..., None, None] * c + jnp.einsum('bshd,bshv->bhdv', kw, vb)
        n_new = wc[..., None] * nv + jnp.sum(kw, axis=1)
        return (c_new, n_new, m_new), h

    (c, nv, m), h = lax.scan(block, (c0, n0, m0), xs)
    return _from_chunks(h), c, nv, m


def _rglru(xin, h0, buf, conv_w, conv_b, wa, ba, wx, bx, lam):
    f32 = jnp.float32
    bsz, s, w = xin.shape
    xx = jnp.concatenate([buf, xin], axis=1)
    conv_w = conv_w.astype(f32)
    xc = conv_b.astype(f32) + sum(xx[:, t:t + s] * conv_w[t] for t in range(RG_CONV))
    new_buf = xx[:, -(RG_CONV - 1):]
    xb = xc.reshape(bsz, s, RG_BLOCKS, RG_BS)
    gate_a = jax.nn.sigmoid(jnp.einsum('bsni,nij->bsnj', xb, wa).reshape(bsz, s, w) + ba)
    gate_x = jax.nn.sigmoid(jnp.einsum('bsni,nij->bsnj', xb, wx).reshape(bsz, s, w) + bx)
    log_a = -RG_C * gate_a * jax.nn.softplus(-lam.astype(f32))
    a = jnp.exp(log_a)
    u = jnp.sqrt(-jnp.expm1(2.0 * log_a)) * (gate_x * xc)
    a_cum, h_in = lax.associative_scan(_lin_combine, (a, u), axis=1)
    h = a_cum * h0[:, None] + h_in
    return h, h[:, -1], new_buf


def _rotary(t, pos):
    half = t.shape[-1] // 2
    inv = 1.0 / (ROPE_BASE ** (jnp.arange(half, dtype=jnp.float32) / half))
    ang = pos.astype(jnp.float32)[:, None] * inv[None, :]
    cos = jnp.cos(ang)[None, :, None, :]
    sin = jnp.sin(ang)[None, :, None, :]
    t1, t2 = t[..., :half], t[..., half:]
    return jnp.concatenate([t1 * cos - t2 * sin, t1 * sin + t2 * cos], axis=-1)


def _retention(q, k, v, s0):
    f32 = jnp.float32
    bsz, s = q.shape[0], q.shape[1]
    L = math.gcd(s, RET_CHUNK)
    n = s // L
    log_gamma = jnp.log1p(-jnp.exp2(-5.0 - jnp.arange(RET_HEADS, dtype=f32)))
    k = k * (RET_DK ** -0.5)
    j = jnp.arange(L, dtype=f32)
    diff = j[:, None] - j[None, :]
    mask = diff >= 0
    dec = jnp.where(mask[..., None], jnp.exp(jnp.where(mask, diff, 0.0)[..., None] * log_gamma), 0.0)
    q_decay = jnp.exp((j[:, None] + 1.0) * log_gamma)
    k_decay = jnp.exp((L - 1.0 - j)[:, None] * log_gamma)
    blk_decay = jnp.exp(L * log_gamma)
    xs = tuple(_to_chunks(t, n, L) for t in (q, k, v))

    def block(st, blk):
        qb, kb, vb = blk
        sc = jnp.einsum('bthd,bshd->btsh', qb, kb) * dec
        o = jnp.einsum('btsh,bshv->bthv', sc, vb) + q_decay[..., None] * jnp.einsum('bthd,bhdv->bthv', qb, st)
        st = blk_decay[:, None, None] * st + jnp.einsum('bshd,bshv->bhdv', kb * k_decay[..., None], vb)
        return st, o

    st, o = lax.scan(block, s0, xs)
    return _from_chunks(o), st


def _run_group(x, pos, s5_re, s5_im, ml_c, ml_n, ml_m, rg_h, rg_conv, ret_s, P):
    f32 = jnp.float32
    bsz, s, _ = x.shape
    h = x.astype(f32)
    o_s5r, o_s5i, o_c, o_n, o_m, o_rh, o_rc, o_rs = [], [], [], [], [], [], [], []
    for layer in range(DEPTH):
        j = layer // 2
        hn = _rmsnorm(h, P['norm_g'][layer])
        if layer % 2 == 0:
            u, z_a, q, k, v, o, ig, fg, z_b = _split(hn @ P['w_in_even'][j], SPLIT_EVEN)
            y_a, sr, si = _s5(u, s5_re[j].astype(f32), s5_im[j].astype(f32),
                              P['s5_lambda_re'][j], P['s5_lambda_im'][j], P['s5_b_re'][j], P['s5_b_im'][j],
                              P['s5_c_re'][j], P['s5_c_im'][j], P['s5_d'][j], P['s5_log_step'][j],
                              P['s5_glu_w'][j], P['s5_glu_b'][j])
            hb, c, nv, m = _mlstm(q.reshape(bsz, s, ML_HEADS, ML_DQK), k.reshape(bsz, s, ML_HEADS, ML_DQK),
                                  v.reshape(bsz, s, ML_HEADS, ML_DV), ig + P['ml_b_i'][j], fg + P['ml_b_f'][j],
                                  ml_c[j].astype(f32), ml_n[j].astype(f32), ml_m[j].astype(f32))
            y_b = _head_rmsnorm(hb, P['ml_norm_g'][j]) * jax.nn.sigmoid(o)
            mixed = jnp.concatenate([y_a * jax.nn.silu(z_a), y_b * jax.nn.silu(z_b)], axis=-1)
            h = h + mixed @ P['w_out_even'][j]
            o_s5r.append(sr)
            o_s5i.append(si)
            o_c.append(c)
            o_n.append(nv)
            o_m.append(m)
        else:
            xin, z_c, q, k, v, z_d = _split(hn @ P['w_in_odd'][j], SPLIT_ODD)
            y_c, h_last, buf = _rglru(xin, rg_h[j].astype(f32), rg_conv[j].astype(f32),
                                      P['rg_conv_w'][j], P['rg_conv_b'][j], P['rg_wa'][j], P['rg_ba'][j],
                                      P['rg_wx'][j], P['rg_bx'][j], P['rg_lambda'][j])
            qr = _rotary(q.reshape(bsz, s, RET_HEADS, RET_DK), pos)
            kr = _rotary(k.reshape(bsz, s, RET_HEADS, RET_DK), pos)
            o_d, st = _retention(qr, kr, v.reshape(bsz, s, RET_HEADS, RET_DV), ret_s[j].astype(f32))
            y_d = _head_layernorm(o_d, P['ret_norm_g'][j])
            mixed = jnp.concatenate([y_c * jax.nn.silu(z_c), y_d * jax.nn.silu(z_d)], axis=-1)
            h = h + mixed @ P['w_out_odd'][j]
            o_rh.append(h_last)
            o_rc.append(buf)
            o_rs.append(st)
    y = _rmsnorm(h, P['final_norm_g']).astype(x.dtype)
    return y, (jnp.stack(o_s5r), jnp.stack(o_s5i), jnp.stack(o_c), jnp.stack(o_n), jnp.stack(o_m),
               jnp.stack(o_rh), jnp.stack(o_rc), jnp.stack(o_rs))


def setup_inputs(seed: int = 0) -> dict:
    key = jax.random.key(seed)
    ks = list(jax.random.split(key, 37))
    f32 = jnp.float32

    def nrm(i, shape, scale):
        return scale * jax.random.normal(ks[i], shape, f32)

    u_lam = jax.random.uniform(ks[35], (N_ODD, BRANCH), f32, minval=0.9, maxval=0.999)
    a_lam = u_lam ** (1.0 / RG_C)
    return {
        'x_prompt': nrm(0, (BATCH, SEQ, D_MODEL), 1.0),
        'x_sample': nrm(1, (DEC_BATCH, DEC_SEQ, D_MODEL), 1.0),
        'state_s5_re': nrm(2, (N_EVEN, DEC_BATCH, S5_G, S5_P), 0.1),
        'state_s5_im': nrm(3, (N_EVEN, DEC_BATCH, S5_G, S5_P), 0.1),
        'state_mlstm_c': nrm(4, (N_EVEN, DEC_BATCH, ML_HEADS, ML_DQK, ML_DV), 0.02),
        'state_mlstm_n': nrm(5, (N_EVEN, DEC_BATCH, ML_HEADS, ML_DQK), 0.1),
        'state_mlstm_m': nrm(6, (N_EVEN, DEC_BATCH, ML_HEADS), 1.0),
        'state_rglru_h': nrm(7, (N_ODD, DEC_BATCH, BRANCH), 0.5),
        'state_rglru_conv': nrm(8, (N_ODD, DEC_BATCH, RG_CONV - 1, BRANCH), 1.0),
        'state_ret_s': nrm(9, (N_ODD, DEC_BATCH, RET_HEADS, RET_DK, RET_DV), 0.05),
        'norm_g': 1.0 + nrm(10, (DEPTH, D_MODEL), 0.02),
        'final_norm_g': 1.0 + nrm(11, (D_MODEL,), 0.02),
        'w_in_even': nrm(12, (N_EVEN, D_MODEL, IN_EVEN), D_MODEL ** -0.5),
        'w_out_even': nrm(13, (N_EVEN, MIX, D_MODEL), MIX ** -0.5),
        's5_lambda_re': -0.5 + nrm(14, (N_EVEN, S5_G, S5_P), 0.01),
        's5_lambda_im': math.pi * jnp.arange(S5_P, dtype=f32)[None, None, :] + nrm(15, (N_EVEN, S5_G, S5_P), 0.01),
        's5_b_re': nrm(16, (N_EVEN, S5_G, S5_P, S5_H), (2.0 * S5_H) ** -0.5),
        's5_b_im': nrm(17, (N_EVEN, S5_G, S5_P, S5_H), (2.0 * S5_H) ** -0.5),
        's5_c_re': nrm(18, (N_EVEN, S5_G, S5_H, S5_P), (2.0 * S5_P) ** -0.5),
        's5_c_im': nrm(19, (N_EVEN, S5_G, S5_H, S5_P), (2.0 * S5_P) ** -0.5),
        's5_d': nrm(20, (N_EVEN, BRANCH), 1.0),
        's5_log_step': jax.random.uniform(ks[21], (N_EVEN, S5_G), f32, minval=math.log(1e-3), maxval=math.log(1e-1)),
        's5_glu_w': nrm(22, (N_EVEN, BRANCH, BRANCH), BRANCH ** -0.5),
        's5_glu_b': nrm(23, (N_EVEN, BRANCH), 0.01),
        'ml_b_i': nrm(24, (N_EVEN, ML_HEADS), 0.1),
        'ml_b_f': jnp.linspace(3.0, 6.0, ML_HEADS, dtype=f32)[None, :] + nrm(25, (N_EVEN, ML_HEADS), 0.01),
        'ml_norm_g': 1.0 + nrm(26, (N_EVEN, BRANCH), 0.02),
        'w_in_odd': nrm(27, (N_ODD, D_MODEL, IN_ODD), D_MODEL ** -0.5),
        'w_out_odd': nrm(28, (N_ODD, MIX, D_MODEL), MIX ** -0.5),
        'rg_conv_w': nrm(29, (N_ODD, RG_CONV, BRANCH), RG_CONV ** -0.5),
        'rg_conv_b': nrm(30, (N_ODD, BRANCH), 0.01),
        'rg_wa': nrm(31, (N_ODD, RG_BLOCKS, RG_BS, RG_BS), RG_BS ** -0.5),
        'rg_ba': nrm(32, (N_ODD, BRANCH), 0.01),
        'rg_wx': nrm(33, (N_ODD, RG_BLOCKS, RG_BS, RG_BS), RG_BS ** -0.5),
        'rg_bx': nrm(34, (N_ODD, BRANCH), 0.01),
        'rg_lambda': jnp.log(a_lam) - jnp.log1p(-a_lam),
        'ret_norm_g': 1.0 + nrm(36, (N_ODD, BRANCH), 0.02),
    }


def reference(x_prompt, x_sample, state_s5_re, state_s5_im, state_mlstm_c, state_mlstm_n, state_mlstm_m,
              state_rglru_h, state_rglru_conv, state_ret_s, norm_g, final_norm_g, w_in_even, w_out_even,
              s5_lambda_re, s5_lambda_im, s5_b_re, s5_b_im, s5_c_re, s5_c_im, s5_d, s5_log_step, s5_glu_w,
              s5_glu_b, ml_b_i, ml_b_f, ml_norm_g, w_in_odd, w_out_odd, rg_conv_w, rg_conv_b, rg_wa, rg_ba,
              rg_wx, rg_bx, rg_lambda, ret_norm_g):
    f32 = jnp.float32
    P = dict(norm_g=norm_g, final_norm_g=final_norm_g, w_in_even=w_in_even, w_out_even=w_out_even,
             s5_lambda_re=s5_lambda_re, s5_lambda_im=s5_lambda_im, s5_b_re=s5_b_re, s5_b_im=s5_b_im,
             s5_c_re=s5_c_re, s5_c_im=s5_c_im, s5_d=s5_d, s5_log_step=s5_log_step, s5_glu_w=s5_glu_w,
             s5_glu_b=s5_glu_b, ml_b_i=ml_b_i, ml_b_f=ml_b_f, ml_norm_g=ml_norm_g, w_in_odd=w_in_odd,
             w_out_odd=w_out_odd, rg_conv_w=rg_conv_w, rg_conv_b=rg_conv_b, rg_wa=rg_wa, rg_ba=rg_ba,
             rg_wx=rg_wx, rg_bx=rg_bx, rg_lambda=rg_lambda, ret_norm_g=ret_norm_g)
    bp = x_prompt.shape[0]
    pos_p = jnp.arange(x_prompt.shape[1])
    pos_s = PAST_LEN + jnp.arange(x_sample.shape[1])
    y_prompt, st_p = _run_group(
        x_prompt, pos_p,
        jnp.zeros((N_EVEN, bp, S5_G, S5_P), f32), jnp.zeros((N_EVEN, bp, S5_G, S5_P), f32),
        jnp.zeros((N_EVEN, bp, ML_HEADS, ML_DQK, ML_DV), f32), jnp.zeros((N_EVEN, bp, ML_HEADS, ML_DQK), f32),
        jnp.zeros((N_EVEN, bp, ML_HEADS), f32), jnp.zeros((N_ODD, bp, BRANCH), f32),
        jnp.zeros((N_ODD, bp, RG_CONV - 1, BRANCH), f32), jnp.zeros((N_ODD, bp, RET_HEADS, RET_DK, RET_DV), f32),
        P)
    y_sample, st_s = _run_group(x_sample, pos_s, state_s5_re, state_s5_im, state_mlstm_c, state_mlstm_n,
                                state_mlstm_m, state_rglru_h, state_rglru_conv, state_ret_s, P)
    s5_re_p, s5_im_p, mlstm_c_p, mlstm_n_p, mlstm_m_p, rglru_h_p, rglru_conv_p, ret_s_p = st_p
    s5_re_s, s5_im_s, mlstm_c_s, mlstm_n_s, mlstm_m_s, rglru_h_s, rglru_conv_s, ret_s_s = st_s
    return (y_prompt, y_sample, s5_re_p, s5_re_s, s5_im_p, s5_im_s, mlstm_c_p, mlstm_c_s, mlstm_n_p, mlstm_n_s,
            mlstm_m_p, mlstm_m_s, rglru_h_p, rglru_h_s, rglru_conv_p, rglru_conv_s, ret_s_p, ret_s_s)
```

```python
import functools
import math

import jax
import jax.numpy as jnp
import numpy as np
from jax import lax
from jax.experimental import pallas as pl
from jax.experimental.pallas import tpu as pltpu

F32 = jnp.float32
BF16 = jnp.bfloat16

EPS = 1e-6
D_MODEL = 4096
BRANCH = 4096
S5_H = 16
S5_P = 64
S5_G = BRANCH // S5_H
S5_GB = 16
S5_NB = S5_G // S5_GB
S5_CB = S5_GB * S5_H
S5_SB = S5_GB * S5_P
SUBLANES = 8
ML_HEADS = 8
ML_DQK = 256
ML_DV = 512
RG_CONV = 4
RG_BLOCKS = 16
RG_BS = 256
RG_C = 8.0
RET_HEADS = 16
RET_DK = 256
RET_DV = 256
ROPE_BASE = 10000.0
PAST_LEN = 16384

CHUNK = 128
STEP_BB = 8
MIB = 1024 * 1024


def _params(sem, vmem_mib):
    return pltpu.CompilerParams(dimension_semantics=sem, vmem_limit_bytes=vmem_mib * MIB)


def _sigmoid(x):
    return 1.0 / (1.0 + jnp.exp(-x))


def _silu(x):
    return x * _sigmoid(x)


def _log_sigmoid(x):
    return jnp.minimum(x, 0.0) - jnp.log(1.0 + jnp.exp(-jnp.abs(x)))


def _softplus(x):
    return jnp.maximum(x, 0.0) + jnp.log(1.0 + jnp.exp(-jnp.abs(x)))


def _gelu_tanh(x):
    c = math.sqrt(2.0 / math.pi)
    return 0.5 * x * (1.0 + jnp.tanh(c * (x + 0.044715 * (x * x * x))))


def _rmsnorm_kernel(x_ref, g_ref, o_ref):
    x = x_ref[...]
    ms = jnp.mean(x * x, axis=-1, keepdims=True)
    o_ref[...] = (x * lax.rsqrt(ms + EPS) * g_ref[...]).astype(o_ref.dtype)


def _rmsnorm(x, g, out_dtype, tm):
    m, d = x.shape
    return pl.pallas_call(
        _rmsnorm_kernel,
        grid=(m // tm,),
        in_specs=[pl.BlockSpec((tm, d), lambda i: (i, 0)),
                  pl.BlockSpec((1, d), lambda i: (0, 0))],
        out_specs=pl.BlockSpec((tm, d), lambda i: (i, 0)),
        out_shape=jax.ShapeDtypeStruct((m, d), out_dtype),
        compiler_params=_params(("parallel",), 40),
        name="rmsnorm",
    )(x, g.reshape(1, d))


def _mm_kernel(a_ref, b_ref, o_ref):
    o_ref[...] = jnp.dot(a_ref[...], b_ref[...], preferred_element_type=F32).astype(o_ref.dtype)


def _matmul(a, b, tm, tn, out_dtype=F32, vmem_mib=56):
    m, k = a.shape
    n = b.shape[1]
    return pl.pallas_call(
        _mm_kernel,
        grid=(m // tm, n // tn),
        in_specs=[pl.BlockSpec((tm, k), lambda i, j: (i, 0)),
                  pl.BlockSpec((k, tn), lambda i, j: (0, j))],
        out_specs=pl.BlockSpec((tm, tn), lambda i, j: (i, j)),
        out_shape=jax.ShapeDtypeStruct((m, n), out_dtype),
        compiler_params=_params(("parallel", "arbitrary"), vmem_mib),
        name="in_proj",
    )(a, b)


def _outproj_kernel(a1_ref, a2_ref, w1_ref, w2_ref, r_ref, o_ref):
    acc = jnp.dot(a1_ref[...], w1_ref[...], preferred_element_type=F32)
    acc = acc + jnp.dot(a2_ref[...], w2_ref[...], preferred_element_type=F32)
    o_ref[...] = r_ref[...] + acc


def _outproj(a1, a2, w, res, tm, tn):
    m, k = a1.shape
    n = w.shape[1]
    return pl.pallas_call(
        _outproj_kernel,
        grid=(m // tm, n // tn),
        in_specs=[pl.BlockSpec((tm, k), lambda i, j: (i, 0)),
                  pl.BlockSpec((tm, k), lambda i, j: (i, 0)),
                  pl.BlockSpec((k, tn), lambda i, j: (0, j)),
                  pl.BlockSpec((k, tn), lambda i, j: (1, j)),
                  pl.BlockSpec((tm, tn), lambda i, j: (i, j))],
        out_specs=pl.BlockSpec((tm, tn), lambda i, j: (i, j)),
        out_shape=jax.ShapeDtypeStruct((m, n), F32),
        compiler_params=_params(("parallel", "arbitrary"), 56),
        name="out_proj",
    )(a1, a2, w, w, res)


def _glu_kernel(g_ref, w_ref, b_ref, z_ref, o_ref, gb_ref, *, tn):
    j = pl.program_id(1)

    @pl.when(j == 0)
    def _():
        gb_ref[...] = g_ref[...].astype(BF16)

    lin = jnp.dot(gb_ref[...], w_ref[...], preferred_element_type=F32) + b_ref[...]
    g = g_ref[:, pl.ds(pl.multiple_of(j * tn, tn), tn)]
    o_ref[...] = (g * _sigmoid(lin) * _silu(z_ref[...])).astype(o_ref.dtype)


def _glu(g, w, b, proj, z_col0, tm, tn):
    m, k = g.shape
    n = w.shape[1]
    zb = z_col0 // tn
    return pl.pallas_call(
        functools.partial(_glu_kernel, tn=tn),
        grid=(m // tm, n // tn),
        in_specs=[pl.BlockSpec((tm, k), lambda i, j: (i, 0)),
                  pl.BlockSpec((k, tn), lambda i, j: (0, j)),
                  pl.BlockSpec((1, tn), lambda i, j: (0, j)),
                  pl.BlockSpec((tm, tn), lambda i, j: (i, zb + j))],
        out_specs=pl.BlockSpec((tm, tn), lambda i, j: (i, j)),
        out_shape=jax.ShapeDtypeStruct((m, n), BF16),
        scratch_shapes=[pltpu.VMEM((tm, k), BF16)],
        compiler_params=_params(("parallel", "arbitrary"), 56),
        name="s5_glu",
    )(g, w, b.reshape(1, n), proj)


def _s5_discretize(lam_re, lam_im, b_re, b_im, c_re, c_im, log_step):
    step = jnp.exp(log_step.astype(F32))[:, None]
    mag = jnp.exp(lam_re * step)
    ang = lam_im * step
    abar_re = mag * jnp.cos(ang)
    abar_im = mag * jnp.sin(ang)
    nr = abar_re - 1.0
    ni = abar_im
    den = lam_re * lam_re + lam_im * lam_im
    fr = (nr * lam_re + ni * lam_im) / den
    fi = (ni * lam_re - nr * lam_im) / den
    bbar_re = fr[..., None] * b_re - fi[..., None] * b_im
    bbar_im = fr[..., None] * b_im + fi[..., None] * b_re
    eye = jnp.eye(S5_GB, dtype=F32)

    def bblock(t):
        t = t.reshape(S5_NB, S5_GB, S5_P, S5_H)
        return jnp.einsum('ngph,gk->nghkp', t, eye).reshape(S5_NB, S5_CB, S5_SB)

    def cblock(t):
        t = t.reshape(S5_NB, S5_GB, S5_H, S5_P)
        return jnp.einsum('nghp,gk->nkpgh', t, eye).reshape(S5_NB, S5_SB, S5_CB)

    bw = jnp.concatenate([bblock(bbar_re), bblock(bbar_im)], axis=-1).astype(BF16)
    cw = jnp.concatenate([cblock(c_re.astype(F32)), -cblock(c_im.astype(F32))], axis=1).astype(BF16)
    return (abar_re.reshape(S5_NB, 1, S5_SB), abar_im.reshape(S5_NB, 1, S5_SB), bw, cw)


def _seg_stride(t_len):
    return t_len // SUBLANES + 1


def _cpow(re, im, n):
    rr, ri = jnp.ones_like(re), jnp.zeros_like(im)
    while n:
        if n & 1:
            rr, ri = rr * re - ri * im, rr * im + ri * re
        re, im = re * re - im * im, 2.0 * re * im
        n >>= 1
    return rr, ri


def _s5_prompt_kernel(u_ref, are_ref, aim_ref, bw_ref, cw_ref, d_ref,
                      g_ref, sre_ref, sim_ref, x_ref, *, t_len, rows):
    nseg = SUBLANES
    seg = _seg_stride(t_len)
    t_pad = nseg * seg
    nslab = 2 * S5_SB // 128
    half = nslab // 2

    for r in range(t_len // rows):
        sl = slice(r * rows, (r + 1) * rows)
        bu = jnp.dot(u_ref[sl, :].astype(BF16), bw_ref[0], preferred_element_type=F32)
        for c in range(nslab):
            x_ref[c, sl, :] = bu[:, c * 128:(c + 1) * 128]
    for c in range(nslab):
        x_ref[c, t_len:t_pad, :] = jnp.zeros((t_pad - t_len, 128), F32)

    ar = [jnp.broadcast_to(are_ref[0][:, c * 128:(c + 1) * 128], (nseg, 128)) for c in range(half)]
    ai = [jnp.broadcast_to(aim_ref[0][:, c * 128:(c + 1) * 128], (nseg, 128)) for c in range(half)]

    def advance(t, xs, store):
        out = [None] * nslab
        for c in range(half):
            idx = pl.ds(t, nseg, stride=seg)
            xr, xi = xs[c], xs[half + c]
            nr = ar[c] * xr - ai[c] * xi + x_ref[c, idx, :]
            ni = ar[c] * xi + ai[c] * xr + x_ref[half + c, idx, :]
            if store:
                x_ref[c, idx, :] = nr
                x_ref[half + c, idx, :] = ni
            out[c], out[half + c] = nr, ni
        return tuple(out)

    zero = tuple(jnp.zeros((nseg, 128), F32) for _ in range(nslab))
    ends = lax.fori_loop(0, seg, lambda t, xs: advance(t, xs, False), zero)
    er = jnp.concatenate(ends[:half], axis=1)
    ei = jnp.concatenate(ends[half:], axis=1)

    pr, pi = _cpow(are_ref[0], aim_ref[0], seg)
    row = lax.broadcasted_iota(jnp.int32, (nseg, S5_SB), 0)
    cr = jnp.zeros((1, S5_SB), F32)
    ci = jnp.zeros((1, S5_SB), F32)
    xr0 = jnp.zeros((nseg, S5_SB), F32)
    xi0 = jnp.zeros((nseg, S5_SB), F32)
    for s in range(1, nseg):
        cr, ci = (pr * cr - pi * ci + er[s - 1:s, :], pr * ci + pi * cr + ei[s - 1:s, :])
        xr0 = jnp.where(row == s, jnp.broadcast_to(cr, (nseg, S5_SB)), xr0)
        xi0 = jnp.where(row == s, jnp.broadcast_to(ci, (nseg, S5_SB)), xi0)
    start = tuple([xr0[:, c * 128:(c + 1) * 128] for c in range(half)]
                  + [xi0[:, c * 128:(c + 1) * 128] for c in range(half)])
    lax.fori_loop(0, seg, lambda t, xs: advance(t, xs, True), start)

    last = slice(t_len - 1, t_len)
    sre_ref[0, 0] = jnp.concatenate([x_ref[c, last, :] for c in range(half)], axis=1)
    sim_ref[0, 0] = jnp.concatenate([x_ref[half + c, last, :] for c in range(half)], axis=1)

    for r in range(t_len // rows):
        sl = slice(r * rows, (r + 1) * rows)
        x = jnp.concatenate([x_ref[c, sl, :].astype(BF16) for c in range(nslab)], axis=1)
        y = jnp.dot(x, cw_ref[0], preferred_element_type=F32)
        g_ref[sl, :] = _gelu_tanh(y + d_ref[...] * u_ref[sl, :])


def _s5_prompt(proj, n_seq, t_len, disc, d):
    are, aim, bw, cw = disc
    m = n_seq * t_len
    kern = functools.partial(_s5_prompt_kernel, t_len=t_len, rows=min(256, t_len))
    return pl.pallas_call(
        kern,
        grid=(n_seq, S5_NB),
        in_specs=[pl.BlockSpec((t_len, S5_CB), lambda b, j: (b, j)),
                  pl.BlockSpec((1, 1, S5_SB), lambda b, j: (j, 0, 0)),
                  pl.BlockSpec((1, 1, S5_SB), lambda b, j: (j, 0, 0)),
                  pl.BlockSpec((1, S5_CB, 2 * S5_SB), lambda b, j: (j, 0, 0)),
                  pl.BlockSpec((1, 2 * S5_SB, S5_CB), lambda b, j: (j, 0, 0)),
                  pl.BlockSpec((1, S5_CB), lambda b, j: (0, j))],
        out_specs=[pl.BlockSpec((t_len, S5_CB), lambda b, j: (b, j)),
                   pl.BlockSpec((1, 1, 1, S5_SB), lambda b, j: (b, j, 0, 0)),
                   pl.BlockSpec((1, 1, 1, S5_SB), lambda b, j: (b, j, 0, 0))],
        out_shape=[jax.ShapeDtypeStruct((m, BRANCH), F32),
                   jax.ShapeDtypeStruct((n_seq, S5_NB, 1, S5_SB), F32),
                   jax.ShapeDtypeStruct((n_seq, S5_NB, 1, S5_SB), F32)],
        scratch_shapes=[pltpu.VMEM((2 * S5_SB // 128, SUBLANES * _seg_stride(t_len), 128), F32)],
        compiler_params=_params(("parallel", "arbitrary"), 48),
        name="s5_prompt",
    )(proj, are, aim, bw, cw, d.reshape(1, BRANCH))


def _s5_step_kernel(u_ref, xr_ref, xi_ref, are_ref, aim_ref, bw_ref, cw_ref, d_ref,
                    g_ref, sre_ref, sim_ref):
    sb = S5_SB
    u = u_ref[...]
    bu = jnp.dot(u.astype(BF16), bw_ref[0], preferred_element_type=F32)
    ar, ai = are_ref[0], aim_ref[0]
    xr0, xi0 = xr_ref[...], xi_ref[...]
    xr = ar * xr0 - ai * xi0 + bu[:, 0:sb]
    xi = ar * xi0 + ai * xr0 + bu[:, sb:2 * sb]
    sre_ref[...] = xr
    sim_ref[...] = xi
    x = jnp.concatenate([xr, xi], axis=1).astype(BF16)
    y = jnp.dot(x, cw_ref[0], preferred_element_type=F32)
    g_ref[...] = _gelu_tanh(y + d_ref[...] * u)


def _s5_step(proj, st_re, st_im, disc, d):
    are, aim, bw, cw = disc
    nb = proj.shape[0]
    return pl.pallas_call(
        _s5_step_kernel,
        grid=(S5_NB,),
        in_specs=[pl.BlockSpec((nb, S5_CB), lambda j: (0, j)),
                  pl.BlockSpec((nb, S5_SB), lambda j: (0, j)),
                  pl.BlockSpec((nb, S5_SB), lambda j: (0, j)),
                  pl.BlockSpec((1, 1, S5_SB), lambda j: (j, 0, 0)),
                  pl.BlockSpec((1, 1, S5_SB), lambda j: (j, 0, 0)),
                  pl.BlockSpec((1, S5_CB, 2 * S5_SB), lambda j: (j, 0, 0)),
                  pl.BlockSpec((1, 2 * S5_SB, S5_CB), lambda j: (j, 0, 0)),
                  pl.BlockSpec((1, S5_CB), lambda j: (0, j))],
        out_specs=[pl.BlockSpec((nb, S5_CB), lambda j: (0, j)),
                   pl.BlockSpec((nb, S5_SB), lambda j: (0, j)),
                   pl.BlockSpec((nb, S5_SB), lambda j: (0, j))],
        out_shape=[jax.ShapeDtypeStruct((nb, BRANCH), F32),
                   jax.ShapeDtypeStruct((nb, S5_G * S5_P), F32),
                   jax.ShapeDtypeStruct((nb, S5_G * S5_P), F32)],
        compiler_params=_params(("parallel",), 32),
        name="s5_step",
    )(proj, st_re, st_im, are, aim, bw, cw, d.reshape(1, BRANCH))


def _mlstm_prompt_kernel(q_ref, k_ref, v_ref, o_ref, z_ref, gc_ref, gr_ref, bc_ref, br_ref, ng_ref,
                         y_ref, c_ref, n_ref, m_ref, *, chunk):
    L = chunk

    @pl.when(pl.program_id(2) == 0)
    def _():
        c_ref[...] = jnp.zeros_like(c_ref)
        n_ref[...] = jnp.zeros_like(n_ref)
        m_ref[...] = jnp.zeros_like(m_ref)

    gc = gc_ref[0] + bc_ref[0]
    gr = gr_ref[0] + br_ref[0]
    li_c, lf_c = gc[:, 0:1], _log_sigmoid(gc[:, 1:2])
    li_r, lf_r = gr[0:1, :], _log_sigmoid(gr[1:2, :])
    t_idx = lax.broadcasted_iota(jnp.int32, (L, L), 0)
    s_idx = lax.broadcasted_iota(jnp.int32, (L, L), 1)
    causal = s_idx <= t_idx
    b_c = jnp.sum(jnp.where(causal, jnp.broadcast_to(lf_r, (L, L)), 0.0), axis=1, keepdims=True)
    b_r = jnp.sum(jnp.where(t_idx <= s_idx, jnp.broadcast_to(lf_c, (L, L)), 0.0), axis=0, keepdims=True)
    m_prev = m_ref[0, 0][:, 0:1]
    dmat = jnp.where(causal, b_c - b_r + li_r, -jnp.inf)
    inter = b_c + m_prev
    m_t = jnp.maximum(inter, jnp.max(dmat, axis=1, keepdims=True))
    w = jnp.exp(dmat - m_t)
    w_inter = jnp.exp(inter - m_t)

    q = q_ref[...].astype(BF16)
    k32 = k_ref[...] * (ML_DQK ** -0.5)
    kb = k32.astype(BF16)
    vb = v_ref[...].astype(BF16)
    c_old = c_ref[0, 0]
    n_old = n_ref[0, 0]
    qk = lax.dot_general(q, kb, (((1,), (1,)), ((), ())), preferred_element_type=F32) * w
    num = jnp.dot(qk.astype(BF16), vb, preferred_element_type=F32)
    num = num + w_inter * jnp.dot(q, c_old.astype(BF16), preferred_element_type=F32)
    den = jnp.sum(qk, axis=1, keepdims=True)
    den = den + w_inter * jnp.sum(q_ref[...] * n_old, axis=1, keepdims=True)
    h = num / jnp.maximum(jnp.abs(den), jnp.exp(-m_t))

    m_new = m_t[L - 1:L, :]
    b_last = b_c[L - 1:L, :]
    ws = jnp.exp(b_last - b_c + li_c - m_new)
    wc = jnp.exp(b_last + m_prev - m_new)
    kw = k32 * ws
    c_ref[0, 0] = wc * c_old + lax.dot_general(kw.astype(BF16), vb, (((0,), (0,)), ((), ())),
                                               preferred_element_type=F32)
    n_ref[0, 0] = wc * n_old + jnp.sum(kw, axis=0, keepdims=True)
    m_ref[0, 0] = jnp.broadcast_to(m_new, m_ref.shape[2:])

    hn = h * lax.rsqrt(jnp.mean(h * h, axis=1, keepdims=True) + EPS) * ng_ref[...]
    y_ref[...] = (hn * _sigmoid(o_ref[...]) * _silu(z_ref[...])).astype(y_ref.dtype)


def _mlstm_prompt(proj, gates, b_i, b_f, norm_g, n_seq, t_len):
    m = n_seq * t_len
    L = min(CHUNK, t_len)
    nc = t_len // L
    hds = ML_HEADS
    g16 = gates[:, :2 * hds].reshape(m, 2, hds)
    gcol = jnp.transpose(g16, (2, 0, 1))
    grow = jnp.transpose(g16, (2, 1, 0))
    bias = jnp.stack([b_i, b_f], axis=-1).astype(F32)
    bcol = bias.reshape(hds, 1, 2)
    brow = bias.reshape(hds, 2, 1)
    qb, kb = 2 * BRANCH // ML_DQK, (2 * BRANCH + hds * ML_DQK) // ML_DQK
    vb, ob, zb = 3 * BRANCH // ML_DV, 4 * BRANCH // ML_DV, 5 * BRANCH // ML_DV
    row = lambda b, h, c: b * nc + c
    return pl.pallas_call(
        functools.partial(_mlstm_prompt_kernel, chunk=L),
        grid=(n_seq, hds, nc),
        in_specs=[pl.BlockSpec((L, ML_DQK), lambda b, h, c: (row(b, h, c), qb + h)),
                  pl.BlockSpec((L, ML_DQK), lambda b, h, c: (row(b, h, c), kb + h)),
                  pl.BlockSpec((L, ML_DV), lambda b, h, c: (row(b, h, c), vb + h)),
                  pl.BlockSpec((L, ML_DV), lambda b, h, c: (row(b, h, c), ob + h)),
                  pl.BlockSpec((L, ML_DV), lambda b, h, c: (row(b, h, c), zb + h)),
                  pl.BlockSpec((1, L, 2), lambda b, h, c: (h, row(b, h, c), 0)),
                  pl.BlockSpec((1, 2, L), lambda b, h, c: (h, 0, row(b, h, c))),
                  pl.BlockSpec((1, 1, 2), lambda b, h, c: (h, 0, 0)),
                  pl.BlockSpec((1, 2, 1), lambda b, h, c: (h, 0, 0)),
                  pl.BlockSpec((1, ML_DV), lambda b, h, c: (0, h))],
        out_specs=[pl.BlockSpec((L, ML_DV), lambda b, h, c: (row(b, h, c), h)),
                   pl.BlockSpec((1, 1, ML_DQK, ML_DV), lambda b, h, c: (b, h, 0, 0)),
                   pl.BlockSpec((1, 1, 1, ML_DQK), lambda b, h, c: (b, h, 0, 0)),
                   pl.BlockSpec((1, 1, 1, 128), lambda b, h, c: (b, h, 0, 0))],
        out_shape=[jax.ShapeDtypeStruct((m, BRANCH), BF16),
                   jax.ShapeDtypeStruct((n_seq, hds, ML_DQK, ML_DV), F32),
                   jax.ShapeDtypeStruct((n_seq, hds, 1, ML_DQK), F32),
                   jax.ShapeDtypeStruct((n_seq, hds, 1, 128), F32)],
        compiler_params=_params(("parallel", "parallel", "arbitrary"), 32),
        name="mlstm_prompt",
    )(proj, proj, proj, proj, proj, gcol, grow, bcol, brow, norm_g.reshape(1, BRANCH))


def _mlstm_step_kernel(qt_ref, kt_ref, v_ref, o_ref, z_ref, gt_ref, bias_ref, c_ref, nt_ref, mt_ref, ng_ref,
                       y_ref, cn_ref, nn_ref, mn_ref):
    bb = STEP_BB
    qt = qt_ref[0, 0]
    kt = kt_ref[0, 0] * (ML_DQK ** -0.5)
    nt = nt_ref[0, 0]
    g = gt_ref[0, 0] + bias_ref[0]
    li, lf = g[0:1, :], _log_sigmoid(g[1:2, :])
    m_prev = mt_ref[0, 0]
    inter = lf + m_prev
    m_t = jnp.maximum(inter, li)
    w = jnp.exp(li - m_t)
    w_inter = jnp.exp(inter - m_t)
    qk = jnp.sum(qt * kt, axis=0, keepdims=True) * w
    den = qk + w_inter * jnp.sum(qt * nt, axis=0, keepdims=True)
    denom = jnp.maximum(jnp.abs(den), jnp.exp(-m_t))
    kw = kt * w
    nn_ref[0, 0] = w_inter * nt + kw
    mn_ref[0, 0] = m_t
    ng = ng_ref[...]
    for j in range(bb):
        c_old = c_ref[j, 0]
        v = v_ref[j:j + 1, :]
        qc = jnp.sum(qt[:, j:j + 1] * c_old, axis=0, keepdims=True)
        wi = w_inter[:, j:j + 1]
        h = (qk[:, j:j + 1] * v + wi * qc) / denom[:, j:j + 1]
        cn_ref[j, 0] = wi * c_old + kw[:, j:j + 1] * v
        hn = h * lax.rsqrt(jnp.mean(h * h, axis=1, keepdims=True) + EPS) * ng
        y_ref[j:j + 1, :] = (hn * _sigmoid(o_ref[j:j + 1, :]) * _silu(z_ref[j:j + 1, :])).astype(y_ref.dtype)


def _to_step_layout(t, heads, dim):
    nb = t.shape[0]
    t = t.reshape(nb // STEP_BB, STEP_BB, heads, dim)
    return jnp.transpose(t, (2, 0, 3, 1))


def _from_step_layout(t):
    hds, nblk, dim, bb = t.shape
    return jnp.transpose(t, (1, 3, 0, 2)).reshape(nblk * bb, hds, dim)


def _mlstm_step(proj, gates, b_i, b_f, norm_g, c0, n0, m0):
    nb = proj.shape[0]
    hds, bb = ML_HEADS, STEP_BB
    q = proj[:, 2 * BRANCH:2 * BRANCH + hds * ML_DQK]
    k = proj[:, 2 * BRANCH + hds * ML_DQK:3 * BRANCH]
    qt = _to_step_layout(q, hds, ML_DQK)
    kt = _to_step_layout(k, hds, ML_DQK)
    gt = _to_step_layout(jnp.transpose(gates[:, :2 * hds].reshape(nb, 2, hds), (0, 2, 1)).reshape(nb, 2 * hds), hds, 2)
    nt = _to_step_layout(n0.reshape(nb, hds * ML_DQK), hds, ML_DQK)
    mt = _to_step_layout(m0, hds, 1)
    bias = jnp.stack([b_i, b_f], axis=-1).astype(F32).reshape(hds, 2, 1)
    vb, ob, zb = 3 * BRANCH // ML_DV, 4 * BRANCH // ML_DV, 5 * BRANCH // ML_DV
    y, cn, nn, mn = pl.pallas_call(
        _mlstm_step_kernel,
        grid=(hds, nb // bb),
        in_specs=[pl.BlockSpec((1, 1, ML_DQK, bb), lambda h, i: (h, i, 0, 0)),
                  pl.BlockSpec((1, 1, ML_DQK, bb), lambda h, i: (h, i, 0, 0)),
                  pl.BlockSpec((bb, ML_DV), lambda h, i: (i, vb + h)),
                  pl.BlockSpec((bb, ML_DV), lambda h, i: (i, ob + h)),
                  pl.BlockSpec((bb, ML_DV), lambda h, i: (i, zb + h)),
                  pl.BlockSpec((1, 1, 2, bb), lambda h, i: (h, i, 0, 0)),
                  pl.BlockSpec((1, 2, 1), lambda h, i: (h, 0, 0)),
                  pl.BlockSpec((bb, 1, ML_DQK, ML_DV), lambda h, i: (i, h, 0, 0)),
                  pl.BlockSpec((1, 1, ML_DQK, bb), lambda h, i: (h, i, 0, 0)),
                  pl.BlockSpec((1, 1, 1, bb), lambda h, i: (h, i, 0, 0)),
                  pl.BlockSpec((1, ML_DV), lambda h, i: (0, h))],
        out_specs=[pl.BlockSpec((bb, ML_DV), lambda h, i: (i, h)),
                   pl.BlockSpec((bb, 1, ML_DQK, ML_DV), lambda h, i: (i, h, 0, 0)),
                   pl.BlockSpec((1, 1, ML_DQK, bb), lambda h, i: (h, i, 0, 0)),
                   pl.BlockSpec((1, 1, 1, bb), lambda h, i: (h, i, 0, 0))],
        out_shape=[jax.ShapeDtypeStruct((nb, BRANCH), BF16),
                   jax.ShapeDtypeStruct(c0.shape, F32),
                   jax.ShapeDtypeStruct(nt.shape, F32),
                   jax.ShapeDtypeStruct(mt.shape, F32)],
        compiler_params=_params(("parallel", "parallel"), 40),
        name="mlstm_step",
    )(qt, kt, proj, proj, proj, gt, bias, c0, nt, mt, norm_g.reshape(1, BRANCH))
    return y, cn, _from_step_layout(nn), _from_step_layout(mn).reshape(nb, hds)


def _rglru_gates(xc, wa_ref, wx_ref, ba_ref, bx_ref, lam_ref):
    xb = xc.astype(BF16)
    gate_a = _sigmoid(jnp.dot(xb, wa_ref[0], preferred_element_type=F32) + ba_ref[...])
    gate_x = _sigmoid(jnp.dot(xb, wx_ref[0], preferred_element_type=F32) + bx_ref[...])
    log_a = -RG_C * gate_a * _softplus(-lam_ref[...])
    a = jnp.exp(log_a)
    u = jnp.sqrt(1.0 - jnp.exp(2.0 * log_a)) * (gate_x * xc)
    return a, u


def _rglru_prompt_kernel(x_ref, z_ref, cw_ref, cb_ref, wa_ref, wx_ref, ba_ref, bx_ref, lam_ref,
                         y_ref, h_ref, buf_ref, xx_ref, a_ref, u_ref, *, t_len, rows):
    nseg = SUBLANES
    seg = _seg_stride(t_len)
    t_pad = nseg * seg
    nslab = RG_BS // 128
    pad = SUBLANES
    cw = cw_ref[...]
    xx_ref[0:pad, :] = jnp.zeros((pad, RG_BS), F32)
    xx_ref[pad:pad + t_len, :] = x_ref[...]
    for r in range(t_len // rows):
        base = pad + r * rows
        xc = cb_ref[...] + cw[3:4, :] * xx_ref[base:base + rows, :]
        for tap in range(RG_CONV - 1):
            off = base - (RG_CONV - 1) + tap
            xc = xc + cw[tap:tap + 1, :] * xx_ref[off:off + rows, :]
        a, u = _rglru_gates(xc, wa_ref, wx_ref, ba_ref, bx_ref, lam_ref)
        for c in range(nslab):
            a_ref[c, r * rows:(r + 1) * rows, :] = a[:, c * 128:(c + 1) * 128]
            u_ref[c, r * rows:(r + 1) * rows, :] = u[:, c * 128:(c + 1) * 128]
    for c in range(nslab):
        a_ref[c, t_len:t_pad, :] = jnp.zeros((t_pad - t_len, 128), F32)
        u_ref[c, t_len:t_pad, :] = jnp.zeros((t_pad - t_len, 128), F32)

    def scan(t, carry):
        idx = pl.ds(t, nseg, stride=seg)
        out = []
        for c in range(nslab):
            h, p = carry[2 * c], carry[2 * c + 1]
            a = a_ref[c, idx, :]
            h = a * h + u_ref[c, idx, :]
            p = a * p
            u_ref[c, idx, :] = h
            a_ref[c, idx, :] = p
            out += [h, p]
        return tuple(out)
    init = tuple([jnp.zeros((nseg, 128), F32), jnp.ones((nseg, 128), F32)] * nslab)
    ends = lax.fori_loop(0, seg, scan, init)

    row = lax.broadcasted_iota(jnp.int32, (nseg, 128), 0)
    h_in = []
    for c in range(nslab):
        e, p_end = ends[2 * c], ends[2 * c + 1]
        cur = jnp.zeros((1, 128), F32)
        acc = jnp.zeros((nseg, 128), F32)
        for s in range(1, nseg):
            cur = p_end[s - 1:s, :] * cur + e[s - 1:s, :]
            acc = jnp.where(row == s, jnp.broadcast_to(cur, (nseg, 128)), acc)
        h_in.append(acc)

    def stitch(t, carry):
        idx = pl.ds(t, nseg, stride=seg)
        for c in range(nslab):
            u_ref[c, idx, :] = u_ref[c, idx, :] + a_ref[c, idx, :] * h_in[c]
        return carry
    lax.fori_loop(0, seg, stitch, 0)

    for r in range(t_len // rows):
        sl = slice(r * rows, (r + 1) * rows)
        h = jnp.concatenate([u_ref[c, sl, :] for c in range(nslab)], axis=1)
        y_ref[sl, :] = (h * _silu(z_ref[sl, :])).astype(y_ref.dtype)
    h_ref[0] = jnp.concatenate([u_ref[c, t_len - 1:t_len, :] for c in range(nslab)], axis=1)
    buf_ref[0] = x_ref[t_len - (RG_CONV - 1):t_len, :]


def _rglru_prompt(proj, conv_w, conv_b, wa, wx, ba, bx, lam, n_seq, t_len):
    m = n_seq * t_len
    nblk = RG_BLOCKS
    vec = lambda t: t.reshape(1, BRANCH).astype(F32)
    kern = functools.partial(_rglru_prompt_kernel, t_len=t_len, rows=min(256, t_len))
    vspec = pl.BlockSpec((1, RG_BS), lambda b, j: (0, j))
    return pl.pallas_call(
        kern,
        grid=(n_seq, nblk),
        in_specs=[pl.BlockSpec((t_len, RG_BS), lambda b, j: (b, j)),
                  pl.BlockSpec((t_len, RG_BS), lambda b, j: (b, nblk + j)),
                  pl.BlockSpec((RG_CONV, RG_BS), lambda b, j: (0, j)),
                  vspec,
                  pl.BlockSpec((1, RG_BS, RG_BS), lambda b, j: (j, 0, 0)),
                  pl.BlockSpec((1, RG_BS, RG_BS), lambda b, j: (j, 0, 0)),
                  vspec, vspec, vspec],
        out_specs=[pl.BlockSpec((t_len, RG_BS), lambda b, j: (b, j)),
                   pl.BlockSpec((1, 1, RG_BS), lambda b, j: (b, 0, j)),
                   pl.BlockSpec((1, RG_CONV - 1, RG_BS), lambda b, j: (b, 0, j))],
        out_shape=[jax.ShapeDtypeStruct((m, BRANCH), BF16),
                   jax.ShapeDtypeStruct((n_seq, 1, BRANCH), F32),
                   jax.ShapeDtypeStruct((n_seq, RG_CONV - 1, BRANCH), F32)],
        scratch_shapes=[pltpu.VMEM((SUBLANES + t_len, RG_BS), F32),
                        pltpu.VMEM((RG_BS // 128, SUBLANES * _seg_stride(t_len), 128), F32),
                        pltpu.VMEM((RG_BS // 128, SUBLANES * _seg_stride(t_len), 128), F32)],
        compiler_params=_params(("parallel", "parallel"), 32),
        name="rglru_prompt",
    )(proj, proj, conv_w.astype(F32), vec(conv_b), wa.astype(BF16), wx.astype(BF16), vec(ba), vec(bx), vec(lam))


def _rglru_step_kernel(x_ref, z_ref, buf_ref, h0_ref, cw_ref, cb_ref, wa_ref, wx_ref, ba_ref, bx_ref, lam_ref,
                       y_ref, h_ref, nbuf_ref):
    cw = cw_ref[...]
    x = x_ref[...]
    xc = cb_ref[...] + cw[3:4, :] * x
    for tap in range(RG_CONV - 1):
        xc = xc + cw[tap:tap + 1, :] * buf_ref[:, tap, :]
    a, u = _rglru_gates(xc, wa_ref, wx_ref, ba_ref, bx_ref, lam_ref)
    h = a * h0_ref[...] + u
    h_ref[...] = h
    y_ref[...] = (h * _silu(z_ref[...])).astype(y_ref.dtype)
    for tap in range(RG_CONV - 2):
        nbuf_ref[:, tap, :] = buf_ref[:, tap + 1, :]
    nbuf_ref[:, RG_CONV - 2, :] = x


def _rglru_step(proj, h0, buf, conv_w, conv_b, wa, wx, ba, bx, lam):
    nb = proj.shape[0]
    nblk = RG_BLOCKS
    vec = lambda t: t.reshape(1, BRANCH).astype(F32)
    vspec = pl.BlockSpec((1, RG_BS), lambda j: (0, j))
    return pl.pallas_call(
        _rglru_step_kernel,
        grid=(nblk,),
        in_specs=[pl.BlockSpec((nb, RG_BS), lambda j: (0, j)),
                  pl.BlockSpec((nb, RG_BS), lambda j: (0, nblk + j)),
                  pl.BlockSpec((nb, RG_CONV - 1, RG_BS), lambda j: (0, 0, j)),
                  pl.BlockSpec((nb, RG_BS), lambda j: (0, j)),
                  pl.BlockSpec((RG_CONV, RG_BS), lambda j: (0, j)),
                  vspec,
                  pl.BlockSpec((1, RG_BS, RG_BS), lambda j: (j, 0, 0)),
                  pl.BlockSpec((1, RG_BS, RG_BS), lambda j: (j, 0, 0)),
                  vspec, vspec, vspec],
        out_specs=[pl.BlockSpec((nb, RG_BS), lambda j: (0, j)),
                   pl.BlockSpec((nb, RG_BS), lambda j: (0, j)),
                   pl.BlockSpec((nb, RG_CONV - 1, RG_BS), lambda j: (0, 0, j))],
        out_shape=[jax.ShapeDtypeStruct((nb, BRANCH), BF16),
                   jax.ShapeDtypeStruct((nb, BRANCH), F32),
                   jax.ShapeDtypeStruct((nb, RG_CONV - 1, BRANCH), F32)],
        compiler_params=_params(("parallel",), 32),
        name="rglru_step",
    )(proj, proj, buf, h0, conv_w.astype(F32), vec(conv_b), wa.astype(BF16), wx.astype(BF16),
      vec(ba), vec(bx), vec(lam))


def _ret_log_gamma():
    return np.log1p(-np.exp2(-5.0 - np.arange(RET_HEADS, dtype=np.float64)))


def _rope_tables(pos):
    half = RET_DK // 2
    inv = 1.0 / (ROPE_BASE ** (jnp.arange(half, dtype=F32) / half))
    ang = pos.astype(F32)[:, None] * inv[None, :]
    return jnp.cos(ang), jnp.sin(ang)


def _rotate(t, cos, sin):
    half = RET_DK // 2
    t1, t2 = t[:, :half], t[:, half:]
    return jnp.concatenate([t1 * cos - t2 * sin, t1 * sin + t2 * cos], axis=1)


def _ret_prompt_kernel(q_ref, k_ref, v_ref, z_ref, cos_ref, sin_ref, dec_ref, qd_ref, kd_ref, bd_ref, ng_ref,
                       y_ref, s_ref):
    @pl.when(pl.program_id(2) == 0)
    def _():
        s_ref[...] = jnp.zeros_like(s_ref)

    cos, sin = cos_ref[...], sin_ref[...]
    qr = _rotate(q_ref[...], cos, sin)
    kr = _rotate(k_ref[...], cos, sin) * (RET_DK ** -0.5)
    qb = qr.astype(BF16)
    vb = v_ref[...].astype(BF16)
    s_old = s_ref[0, 0]
    sc = lax.dot_general(qb, kr.astype(BF16), (((1,), (1,)), ((), ())), preferred_element_type=F32) * dec_ref[0]
    o = jnp.dot(sc.astype(BF16), vb, preferred_element_type=F32)
    o = o + qd_ref[0] * jnp.dot(qb, s_old.astype(BF16), preferred_element_type=F32)
    kd = (kr * kd_ref[0]).astype(BF16)
    s_ref[0, 0] = bd_ref[0] * s_old + lax.dot_general(kd, vb, (((0,), (0,)), ((), ())),
                                                      preferred_element_type=F32)
    tc = o - jnp.mean(o, axis=1, keepdims=True)
    yn = tc * lax.rsqrt(jnp.mean(tc * tc, axis=1, keepdims=True) + EPS) * ng_ref[...]
    y_ref[...] = (yn * _silu(z_ref[...])).astype(y_ref.dtype)


def _ret_prompt(proj, norm_g, n_seq, t_len):
    m = n_seq * t_len
    L = min(CHUNK, t_len)
    nc = t_len // L
    hds = RET_HEADS
    cos, sin = _rope_tables(jnp.arange(t_len))
    lg = jnp.asarray(_ret_log_gamma(), F32)
    j = jnp.arange(L, dtype=F32)
    diff = j[:, None] - j[None, :]
    mask = diff >= 0
    dec = jnp.where(mask[None], jnp.exp(jnp.where(mask, diff, 0.0)[None] * lg[:, None, None]), 0.0)
    ones = jnp.ones((1, 1, RET_DK), F32)
    qd = jnp.exp((j[None, :] + 1.0) * lg[:, None])[..., None] * ones
    kd = jnp.exp((L - 1.0 - j)[None, :] * lg[:, None])[..., None] * ones
    bd = jnp.exp(L * lg)[:, None, None] * ones
    qb, kb, vb, zb = 2 * hds, 3 * hds, 4 * hds, 5 * hds
    row = lambda b, h, c: b * nc + c
    tab = lambda rows: pl.BlockSpec((1, rows, RET_DK), lambda b, h, c: (h, 0, 0))
    return pl.pallas_call(
        _ret_prompt_kernel,
        grid=(n_seq, hds, nc),
        in_specs=[pl.BlockSpec((L, RET_DK), lambda b, h, c: (row(b, h, c), qb + h)),
                  pl.BlockSpec((L, RET_DK), lambda b, h, c: (row(b, h, c), kb + h)),
                  pl.BlockSpec((L, RET_DV), lambda b, h, c: (row(b, h, c), vb + h)),
                  pl.BlockSpec((L, RET_DV), lambda b, h, c: (row(b, h, c), zb + h)),
                  pl.BlockSpec((L, RET_DK // 2), lambda b, h, c: (c, 0)),
                  pl.BlockSpec((L, RET_DK // 2), lambda b, h, c: (c, 0)),
                  pl.BlockSpec((1, L, L), lambda b, h, c: (h, 0, 0)),
                  tab(L), tab(L), tab(1),
                  pl.BlockSpec((1, RET_DV), lambda b, h, c: (0, h))],
        out_specs=[pl.BlockSpec((L, RET_DV), lambda b, h, c: (row(b, h, c), h)),
                   pl.BlockSpec((1, 1, RET_DK, RET_DV), lambda b, h, c: (b, h, 0, 0))],
        out_shape=[jax.ShapeDtypeStruct((m, BRANCH), BF16),
                   jax.ShapeDtypeStruct((n_seq, hds, RET_DK, RET_DV), F32)],
        compiler_params=_params(("parallel", "parallel", "arbitrary"), 32),
        name="ret_prompt",
    )(proj, proj, proj, proj, cos, sin, dec, qd, kd, bd, norm_g.reshape(1, BRANCH))


def _ret_step_kernel(qt_ref, kt_ref, v_ref, z_ref, cos_ref, sin_ref, gam_ref, s_ref, ng_ref, y_ref, sn_ref):
    bb = STEP_BB
    half = RET_DK // 2
    cos, sin = cos_ref[...], sin_ref[...]

    def rot(t):
        t1, t2 = t[:half, :], t[half:, :]
        return jnp.concatenate([t1 * cos - t2 * sin, t1 * sin + t2 * cos], axis=0)
    qt = rot(qt_ref[0, 0])
    kt = rot(kt_ref[0, 0]) * (RET_DK ** -0.5)
    gam = gam_ref[0][:, 0:1]
    sc = jnp.sum(qt * kt, axis=0, keepdims=True)
    ng = ng_ref[...]
    for j in range(bb):
        s_old = s_ref[j, 0]
        v = v_ref[j:j + 1, :]
        qs = jnp.sum(qt[:, j:j + 1] * s_old, axis=0, keepdims=True)
        o = sc[:, j:j + 1] * v + gam * qs
        sn_ref[j, 0] = gam * s_old + kt[:, j:j + 1] * v
        tc = o - jnp.mean(o, axis=1, keepdims=True)
        yn = tc * lax.rsqrt(jnp.mean(tc * tc, axis=1, keepdims=True) + EPS) * ng
        y_ref[j:j + 1, :] = (yn * _silu(z_ref[j:j + 1, :])).astype(y_ref.dtype)


def _ret_step(proj, norm_g, s0, pos):
    nb = proj.shape[0]
    hds, bb = RET_HEADS, STEP_BB
    qt = _to_step_layout(proj[:, 2 * BRANCH:3 * BRANCH], hds, RET_DK)
    kt = _to_step_layout(proj[:, 3 * BRANCH:4 * BRANCH], hds, RET_DK)
    cos, sin = _rope_tables(jnp.full((bb,), pos))
    cos, sin = cos.T, sin.T
    gam = jnp.broadcast_to(jnp.asarray(np.exp(_ret_log_gamma()), F32)[:, None, None], (hds, 1, 128))
    vb, zb = 4 * hds, 5 * hds
    half = RET_DK // 2
    return pl.pallas_call(
        _ret_step_kernel,
        grid=(hds, nb // bb),
        in_specs=[pl.BlockSpec((1, 1, RET_DK, bb), lambda h, i: (h, i, 0, 0)),
                  pl.BlockSpec((1, 1, RET_DK, bb), lambda h, i: (h, i, 0, 0)),
                  pl.BlockSpec((bb, RET_DV), lambda h, i: (i, vb + h)),
                  pl.BlockSpec((bb, RET_DV), lambda h, i: (i, zb + h)),
                  pl.BlockSpec((half, bb), lambda h, i: (0, 0)),
                  pl.BlockSpec((half, bb), lambda h, i: (0, 0)),
                  pl.BlockSpec((1, 1, 128), lambda h, i: (h, 0, 0)),
                  pl.BlockSpec((bb, 1, RET_DK, RET_DV), lambda h, i: (i, h, 0, 0)),
                  pl.BlockSpec((1, RET_DV), lambda h, i: (0, h))],
        out_specs=[pl.BlockSpec((bb, RET_DV), lambda h, i: (i, h)),
                   pl.BlockSpec((bb, 1, RET_DK, RET_DV), lambda h, i: (i, h, 0, 0))],
        out_shape=[jax.ShapeDtypeStruct((nb, BRANCH), BF16),
                   jax.ShapeDtypeStruct(s0.shape, F32)],
        compiler_params=_params(("parallel", "parallel"), 32),
        name="ret_step",
    )(qt, kt, proj, proj, cos, sin, gam, s0, norm_g.reshape(1, BRANCH))


def _even_weights(w_in):
    g0 = 5 * BRANCH
    g1 = g0 + 2 * ML_HEADS
    main = jnp.concatenate([w_in[:, :g0], w_in[:, g1:]], axis=1).astype(BF16)
    gate = jnp.pad(w_in[:, g0:g1], ((0, 0), (0, 128 - 2 * ML_HEADS))).astype(BF16)
    return main, gate


def kernel(x_prompt, x_sample, state_s5_re, state_s5_im, state_mlstm_c, state_mlstm_n, state_mlstm_m, state_rglru_h, state_rglru_conv, state_ret_s, norm_g, final_norm_g, w_in_even, w_out_even, s5_lambda_re, s5_lambda_im, s5_b_re, s5_b_im, s5_c_re, s5_c_im, s5_d, s5_log_step, s5_glu_w, s5_glu_b, ml_b_i, ml_b_f, ml_norm_g, w_in_odd, w_out_odd, rg_conv_w, rg_conv_b, rg_wa, rg_ba, rg_wx, rg_bx, rg_lambda, ret_norm_g):
    bp, t_len, d = x_prompt.shape
    bs = x_sample.shape[0]
    depth = norm_g.shape[0]
    mp = bp * t_len
    hp = x_prompt.reshape(mp, d).astype(F32)
    hs = x_sample.reshape(bs, d).astype(F32)
    tm_p = min(1024, mp)

    outs = {k: ([], []) for k in ("s5r", "s5i", "c", "n", "m", "rh", "rc", "rs")}
    for layer in range(depth):
        j = layer // 2
        np_ = _rmsnorm(hp, norm_g[layer], BF16, min(512, mp))
        ns_ = _rmsnorm(hs, norm_g[layer], BF16, bs)
        if layer % 2 == 0:
            w_main, w_gate = _even_weights(w_in_even[j])
            w_out = w_out_even[j].astype(BF16)
            proj_p = _matmul(np_, w_main, tm_p, 1024)
            proj_s = _matmul(ns_, w_main, bs, 1024)
            gates_p = _matmul(np_, w_gate, tm_p, 128)
            gates_s = _matmul(ns_, w_gate, bs, 128)
            disc = _s5_discretize(s5_lambda_re[j].astype(F32), s5_lambda_im[j].astype(F32),
                                  s5_b_re[j].astype(F32), s5_b_im[j].astype(F32),
                                  s5_c_re[j], s5_c_im[j], s5_log_step[j])
            glu_w = s5_glu_w[j].astype(BF16)
            g_p, sre_p, sim_p = _s5_prompt(proj_p, bp, t_len, disc, s5_d[j])
            ya_p = _glu(g_p, glu_w, s5_glu_b[j], proj_p, BRANCH, min(512, mp), 512)
            yb_p, c_p, n_p, m_p = _mlstm_prompt(proj_p, gates_p, ml_b_i[j], ml_b_f[j], ml_norm_g[j], bp, t_len)
            g_s, sre_s, sim_s = _s5_step(proj_s, state_s5_re[j].reshape(bs, -1).astype(F32),
                                         state_s5_im[j].reshape(bs, -1).astype(F32), disc, s5_d[j])
            ya_s = _glu(g_s, glu_w, s5_glu_b[j], proj_s, BRANCH, bs, 512)
            yb_s, c_s, n_s, m_s = _mlstm_step(proj_s, gates_s, ml_b_i[j], ml_b_f[j], ml_norm_g[j],
                                              state_mlstm_c[j].astype(F32), state_mlstm_n[j].astype(F32),
                                              state_mlstm_m[j].astype(F32))
            outs["s5r"][0].append(sre_p.reshape(bp, S5_G, S5_P))
            outs["s5i"][0].append(sim_p.reshape(bp, S5_G, S5_P))
            outs["s5r"][1].append(sre_s.reshape(bs, S5_G, S5_P))
            outs["s5i"][1].append(sim_s.reshape(bs, S5_G, S5_P))
            outs["c"][0].append(c_p)
            outs["c"][1].append(c_s)
            outs["n"][0].append(n_p.reshape(bp, ML_HEADS, ML_DQK))
            outs["n"][1].append(n_s)
            outs["m"][0].append(m_p[:, :, 0, 0])
            outs["m"][1].append(m_s)
        else:
            w_main = w_in_odd[j].astype(BF16)
            w_out = w_out_odd[j].astype(BF16)
            proj_p = _matmul(np_, w_main, tm_p, 1024)
            proj_s = _matmul(ns_, w_main, bs, 1024)
            rg = (rg_conv_w[j], rg_conv_b[j], rg_wa[j], rg_wx[j], rg_ba[j], rg_bx[j], rg_lambda[j])
            ya_p, rh_p, rc_p = _rglru_prompt(proj_p, *rg, bp, t_len)
            yb_p, rs_p = _ret_prompt(proj_p, ret_norm_g[j], bp, t_len)
            ya_s, rh_s, rc_s = _rglru_step(proj_s, state_rglru_h[j].astype(F32),
                                           state_rglru_conv[j].astype(F32), *rg)
            yb_s, rs_s = _ret_step(proj_s, ret_norm_g[j], state_ret_s[j].astype(F32), PAST_LEN)
            outs["rh"][0].append(rh_p.reshape(bp, BRANCH))
            outs["rh"][1].append(rh_s)
            outs["rc"][0].append(rc_p)
            outs["rc"][1].append(rc_s)
            outs["rs"][0].append(rs_p)
            outs["rs"][1].append(rs_s)
        hp = _outproj(ya_p, yb_p, w_out, hp, tm_p, 256)
        hs = _outproj(ya_s, yb_s, w_out, hs, bs, 512)

    y_p = _rmsnorm(hp, final_norm_g, x_prompt.dtype, min(512, mp)).reshape(bp, t_len, d)
    y_s = _rmsnorm(hs, final_norm_g, x_sample.dtype, bs).reshape(bs, 1, d)
    res = [y_p, y_s]
    for key in ("s5r", "s5i", "c", "n", "m", "rh", "rc", "rs"):
        res.append(jnp.stack(outs[key][0]))
        res.append(jnp.stack(outs[key][1]))
    return tuple(res)
```
